```python
import math
import jax
import jax.numpy as jnp
from jax import lax
import numpy as np

D_MODEL = 1024
BATCH = 4
SEQ = 4096
DEPTH = 4

N_HEADS = 16
HEAD_DIM = 64
A_HEADS = 8
B_HEADS = 8
MOBA_BLOCK = 256
MOBA_TOPK = 3
MOBA_Q_CHUNK = 64
IDX_HEADS = 16
IDX_DIM = 64
DSA_TOPK_MAX = 256
DSA_Q_CHUNK = 128
DILATED_PATTERNS = ((128, 1), (512, 4), (2048, 16))
REL_BUCKETS = 32
REL_MAX_DIST = 128
D_FF = 2816
CONV_WIDTH = 3
RMS_EPS = 1e-6

kernel_name = 'hybrid_moba_dsa_dilated_convffn'


def rmsnorm(x, g):
    x32 = x.astype(jnp.float32)
    y = x32 * lax.rsqrt(jnp.mean(x32 * x32, axis=-1, keepdims=True) + RMS_EPS)
    return (y * g.astype(jnp.float32)).astype(x.dtype)


def t5_bucket(dist):
    n = jnp.maximum(dist, 0)
    max_exact = REL_BUCKETS // 2
    nf = jnp.maximum(n, max_exact).astype(jnp.float32)
    large = max_exact + (jnp.log(nf / max_exact) / math.log(REL_MAX_DIST / max_exact)
                         * (REL_BUCKETS - max_exact)).astype(jnp.int32)
    large = jnp.minimum(large, REL_BUCKETS - 1)
    return jnp.where(n < max_exact, n, large)


def split_cols(a, sizes):
    out, off = [], 0
    for s in sizes:
        out.append(a[..., off:off + s])
        off += s
    return out


def moba_attention(q, k, v, table_hk):
    B, H, S, Dh = q.shape
    nb = -(-S // MOBA_BLOCK)
    pad = nb * MOBA_BLOCK - S
    kb = jnp.pad(k, ((0, 0), (0, 0), (0, pad), (0, 0))).reshape(B, H, nb, MOBA_BLOCK, Dh)
    vb = jnp.pad(v, ((0, 0), (0, 0), (0, pad), (0, 0))).reshape(B, H, nb, MOBA_BLOCK, Dh)
    kmean = jnp.mean(kb.astype(jnp.float32), axis=3).astype(k.dtype)
    n_sel = min(MOBA_TOPK, nb - 1)
    scale = Dh ** -0.5
    nq = S // MOBA_Q_CHUNK
    qc = jnp.moveaxis(q.reshape(B, H, nq, MOBA_Q_CHUNK, Dh), 2, 0)
    bi = jnp.arange(B)[:, None, None, None]
    hi = jnp.arange(H)[None, :, None, None]
    offs = jnp.arange(MOBA_BLOCK)
    blk_ids = jnp.arange(nb)

    def chunk(args):
        qi, c = args
        q0 = c * MOBA_Q_CHUNK
        t = q0 + jnp.arange(MOBA_Q_CHUNK)
        own = q0 // MOBA_BLOCK
        k_own = lax.dynamic_index_in_dim(kb, own, axis=2, keepdims=False)
        v_own = lax.dynamic_index_in_dim(vb, own, axis=2, keepdims=False)
        d_own = t[:, None] - (own * MOBA_BLOCK + offs)[None, :]
        l_own = jnp.einsum('bhqd,bhkd->bhqk', qi, k_own).astype(jnp.float32) * scale
        l_own = jnp.where(d_own >= 0, l_own + table_hk[:, t5_bucket(d_own)], -jnp.inf)
        if n_sel == 0:
            p = jax.nn.softmax(l_own, axis=-1).astype(v.dtype)
            return jnp.einsum('bhqk,bhkd->bhqd', p, v_own)
        gate = jnp.einsum('bhqd,bhnd->bhqn', qi, kmean).astype(jnp.float32)
        gate = jnp.where(blk_ids < own, gate, -jnp.inf)
        _, sel = lax.top_k(gate, n_sel)
        k_sel = kb[bi, hi, sel]
        v_sel = vb[bi, hi, sel]
        d_sel = t[None, None, :, None, None] - (sel[..., None] * MOBA_BLOCK + offs)
        l_sel = jnp.einsum('bhqd,bhqnkd->bhqnk', qi, k_sel).astype(jnp.float32) * scale
        l_sel = l_sel + table_hk[hi[..., None], t5_bucket(d_sel)]
        l_sel = jnp.where((sel < own)[..., None], l_sel, -jnp.inf)
        n_k = n_sel * MOBA_BLOCK
        logits = jnp.concatenate([l_sel.reshape(B, H, MOBA_Q_CHUNK, n_k), l_own], axis=-1)
        p = jax.nn.softmax(logits, axis=-1).astype(v.dtype)
        p_sel = p[..., :n_k].reshape(B, H, MOBA_Q_CHUNK, n_sel, MOBA_BLOCK)
        p_own = p[..., n_k:]
        return (jnp.einsum('bhqnk,bhqnkd->bhqd', p_sel, v_sel)
                + jnp.einsum('bhqk,bhkd->bhqd', p_own, v_own))

    out = lax.map(chunk, (qc, jnp.arange(nq, dtype=jnp.int32)))
    return jnp.moveaxis(out, 0, 2).reshape(B, H, S, Dh)


def dsa_attention(q, k, v, q_idx, k_idx, w_idx, table_hk):
    B, S, H, Dh = q.shape
    n_keep = min(DSA_TOPK_MAX, S // 4)
    nq = S // DSA_Q_CHUNK
    scale = Dh ** -0.5
    key_pos = jnp.arange(S)
    bi = jnp.arange(B)[:, None, None]

    def chunkify(a):
        return jnp.moveaxis(a.reshape((B, nq, DSA_Q_CHUNK) + a.shape[2:]), 1, 0)

    def chunk(args):
        qi, qii, wi, c = args
        t = c * DSA_Q_CHUNK + jnp.arange(DSA_Q_CHUNK)
        rel = jax.nn.relu(jnp.einsum('bqhd,bsd->bqhs', qii, k_idx) * (IDX_DIM ** -0.5))
        score = jnp.einsum('bqhs,bqh->bqs', rel, wi).astype(jnp.float32)
        score = jnp.where(key_pos[None, :] <= t[:, None], score, -jnp.inf)
        _, sel = lax.top_k(score, n_keep)
        k_sel = k[bi, sel]
        v_sel = v[bi, sel]
        d = t[None, :, None] - sel
        logits = jnp.einsum('bqhd,bqkhd->bhqk', qi, k_sel).astype(jnp.float32) * scale
        logits = logits + jnp.moveaxis(table_hk[:, t5_bucket(d)], 0, 1)
        logits = jnp.where((d >= 0)[:, None], logits, -jnp.inf)
        p = jax.nn.softmax(logits, axis=-1).astype(v.dtype)
        return jnp.einsum('bhqk,bqkhd->bqhd', p, v_sel)

    out = lax.map(chunk, (chunkify(q), chunkify(q_idx), chunkify(w_idx),
                          jnp.arange(nq, dtype=jnp.int32)))
    return jnp.moveaxis(out, 0, 1).reshape(B, S, H, Dh)


def dilated_branch(q, k, v, table_hk, window, dilation):
    B, H, S, Dh = q.shape
    w_sub = window // dilation
    bb = w_sub
    n = S // dilation
    nbk = -(-n // bb)
    npad = nbk * bb
    scale = Dh ** -0.5

    def to_sub(a):
        a = jnp.swapaxes(a.reshape(B, H, n, dilation, Dh), 2, 3)
        return jnp.pad(a, ((0, 0), (0, 0), (0, 0), (0, npad - n), (0, 0)))

    def band(a):
        ap = jnp.pad(a, ((0, 0), (0, 0), (0, 0), (bb, 0), (0, 0)))
        prev = ap[:, :, :, :npad].reshape(B, H, dilation, nbk, bb, Dh)
        cur = a.reshape(B, H, dilation, nbk, bb, Dh)
        return jnp.concatenate([prev, cur], axis=4)

    qs = to_sub(q).reshape(B, H, dilation, nbk, bb, Dh)
    kband = band(to_sub(k))
    vband = band(to_sub(v))
    logits = jnp.einsum('bhrnqd,bhrnkd->bhrnqk', qs, kband).astype(jnp.float32) * scale
    a_i = jnp.arange(bb)
    m_i = jnp.arange(2 * bb)
    di = a_i[:, None] + bb - m_i[None, :]
    band_ok = (di >= 0) & (di <= w_sub)
    j_idx = jnp.arange(nbk)[:, None] * bb - bb + m_i[None, :]
    mask = band_ok[None] & (j_idx >= 0)[:, None, :]
    bias = table_hk[:, t5_bucket(di * dilation)][:, None, None]
    logits = jnp.where(mask, logits + bias, -jnp.inf)
    mx = jnp.max(logits, axis=-1, keepdims=True)
    e = jnp.exp(logits - mx)
    den = jnp.sum(e, axis=-1, keepdims=True)
    o = jnp.einsum('bhrnqk,bhrnkd->bhrnqd', (e / den).astype(v.dtype), vband)
    lse = (mx + jnp.log(den))[..., 0]
    o = jnp.swapaxes(o.reshape(B, H, dilation, npad, Dh)[:, :, :, :n], 2, 3).reshape(B, H, S, Dh)
    lse = jnp.swapaxes(lse.reshape(B, H, dilation, npad)[:, :, :, :n], 2, 3).reshape(B, H, S)
    return o, lse


def dilated_mixture(q, k, v, table_hk):
    outs, lses = [], []
    for window, dilation in DILATED_PATTERNS:
        o, l = dilated_branch(q, k, v, table_hk, window, dilation)
        outs.append(o)
        lses.append(l)
    alpha = jax.nn.softmax(jnp.stack(lses, axis=0), axis=0)
    return jnp.einsum('pbhs,pbhsd->bhsd', alpha.astype(v.dtype), jnp.stack(outs, axis=0))


def conv_ffn(h, w_up, conv_w, conv_b, w_down):
    S = h.shape[1]
    u = h @ w_up
    up = jnp.pad(u, ((0, 0), (CONV_WIDTH - 1, 0), (0, 0)))
    c = conv_b
    for j in range(CONV_WIDTH):
        c = c + conv_w[j] * up[:, j:j + S]
    val, gate = jnp.split(c, 2, axis=-1)
    return (jax.nn.silu(gate) * val) @ w_down


def setup_inputs(seed: int = 0) -> dict:
    key = jax.random.key(seed)
    ks = jax.random.split(key, 13)
    n_even = (DEPTH + 1) // 2
    n_odd = DEPTH // 2
    even_in = 3 * A_HEADS * HEAD_DIM + 3 * B_HEADS * HEAD_DIM + IDX_HEADS * IDX_DIM + IDX_DIM + IDX_HEADS
    odd_in = 3 * N_HEADS * HEAD_DIM
    mix = N_HEADS * HEAD_DIM

    def nrm(k, shape, s):
        return jax.random.normal(k, shape, jnp.float32) * s

    return {
        'x': nrm(ks[0], (BATCH, SEQ, D_MODEL), 1.0),
        'w_in_even': nrm(ks[1], (n_even, D_MODEL, even_in), D_MODEL ** -0.5),
        'idx_k_norm': 1.0 + nrm(ks[2], (n_even, IDX_DIM), 0.02),
        'w_in_odd': nrm(ks[3], (n_odd, D_MODEL, odd_in), D_MODEL ** -0.5),
        'w_o': nrm(ks[4], (DEPTH, mix, D_MODEL), mix ** -0.5),
        'rel_bias': nrm(ks[5], (REL_BUCKETS, N_HEADS), 0.2),
        'attn_norm': 1.0 + nrm(ks[6], (DEPTH, D_MODEL), 0.02),
        'ffn_norm': 1.0 + nrm(ks[7], (DEPTH, D_MODEL), 0.02),
        'w_up': nrm(ks[8], (DEPTH, D_MODEL, 2 * D_FF), D_MODEL ** -0.5),
        'conv_w': nrm(ks[9], (DEPTH, CONV_WIDTH, 2 * D_FF), CONV_WIDTH ** -0.5),
        'conv_b': nrm(ks[10], (DEPTH, 2 * D_FF), 0.02),
        'w_down': nrm(ks[11], (DEPTH, D_FF, D_MODEL), D_FF ** -0.5),
        'final_norm': 1.0 + nrm(ks[12], (D_MODEL,), 0.02),
    }


def reference(x, w_in_even, idx_k_norm, w_in_odd, w_o, rel_bias, attn_norm, ffn_norm,
              w_up, conv_w, conv_b, w_down, final_norm):
    B, S, _ = x.shape
    mix = N_HEADS * HEAD_DIM
    sizes_even = [A_HEADS * HEAD_DIM] * 3 + [B_HEADS * HEAD_DIM] * 3 + [IDX_HEADS * IDX_DIM, IDX_DIM, IDX_HEADS]
    sizes_odd = [N_HEADS * HEAD_DIM] * 3
    table_a = rel_bias[:, :A_HEADS].T
    table_b = rel_bias[:, A_HEADS:A_HEADS + B_HEADS].T
    table_c = rel_bias.T
    for layer in range(DEPTH):
        h = rmsnorm(x, attn_norm[layer])
        if layer % 2 == 0:
            j = layer // 2
            qa, ka, va, qb, kb, vb, qi, ki, wi = split_cols(h @ w_in_even[j], sizes_even)
            qa, ka, va = (jnp.swapaxes(a.reshape(B, S, A_HEADS, HEAD_DIM), 1, 2) for a in (qa, ka, va))
            o_a = jnp.swapaxes(moba_attention(qa, ka, va, table_a), 1, 2)
            qb, kb, vb = (a.reshape(B, S, B_HEADS, HEAD_DIM) for a in (qb, kb, vb))
            o_b = dsa_attention(qb, kb, vb,
                                qi.reshape(B, S, IDX_HEADS, IDX_DIM),
                                rmsnorm(ki, idx_k_norm[j]),
                                wi * (IDX_HEADS ** -0.5),
                                table_b)
            mixed = jnp.concatenate([o_a, o_b], axis=2).reshape(B, S, mix)
        else:
            j = layer // 2
            qc, kc, vc = (jnp.swapaxes(a.reshape(B, S, N_HEADS, HEAD_DIM), 1, 2)
                          for a in split_cols(h @ w_in_odd[j], sizes_odd))
            o_c = dilated_mixture(qc, kc, vc, table_c)
            mixed = jnp.swapaxes(o_c, 1, 2).reshape(B, S, mix)
        x = x + mixed @ w_o[layer]
        h = rmsnorm(x, ffn_norm[layer])
        x = x + conv_ffn(h, w_up[layer], conv_w[layer], conv_b[layer], w_down[layer])
    return rmsnorm(x, final_norm)
```

```python
import functools
import math

import jax
import jax.numpy as jnp
import numpy as np
from jax import lax
from jax.experimental import pallas as pl
from jax.experimental.pallas import tpu as pltpu

N_HEADS = 16
HEAD_DIM = 64
A_HEADS = 8
B_HEADS = 8
MOBA_BLOCK = 256
MOBA_TOPK = 3
IDX_HEADS = 16
IDX_DIM = 64
DSA_TOPK_MAX = 256
DILATED_PATTERNS = ((128, 1), (512, 4), (2048, 16))
REL_BUCKETS = 32
REL_MAX_DIST = 128
CONV_WIDTH = 3
RMS_EPS = 1e-6

F32 = jnp.float32
BF16 = jnp.bfloat16
NEG = -1e30
SCALE = HEAD_DIM ** -0.5
LANES = 128
ROW_TILE = 256
ATT_TILE = 256
VMEM_LIMIT = 56 * 1024 * 1024


def _cparams(n_axes):
    return pltpu.CompilerParams(dimension_semantics=("arbitrary",) * n_axes, vmem_limit_bytes=VMEM_LIMIT)


def _dot(a, b):
    return jnp.dot(a, b, preferred_element_type=F32)


def _dot_t(a, b):
    return lax.dot_general(a, b, (((1,), (1,)), ((), ())), preferred_element_type=F32)


def _rms(x, g):
    ms = jnp.mean(x * x, axis=-1, keepdims=True)
    return x * lax.rsqrt(ms + RMS_EPS) * g


def _t5_bucket_np(dist):
    n = np.maximum(dist, 0)
    max_exact = REL_BUCKETS // 2
    nf = np.maximum(n, max_exact).astype(np.float32)
    ratio = np.log(nf / np.float32(max_exact)) / np.float32(math.log(REL_MAX_DIST / max_exact))
    large = max_exact + (ratio * np.float32(REL_BUCKETS - max_exact)).astype(np.int32)
    large = np.minimum(large, REL_BUCKETS - 1)
    return np.where(n < max_exact, n, large).astype(np.int32)


def _near_bucket_tiles(t):
    i = np.arange(t)[:, None]
    j = np.arange(t)[None, :]
    sub = _t5_bucket_np(i - j + t)
    dg = np.where(i - j >= 0, _t5_bucket_np(i - j), -1)
    return np.stack([sub, dg]).astype(np.int32)


def _dilated_bucket_tiles():
    tiles = []
    for window, dilation in DILATED_PATTERNS:
        bb = window // dilation
        a = np.arange(bb)[:, None]
        m = np.arange(2 * bb)[None, :]
        di = a + bb - m
        ok = (di >= 0) & (di <= bb)
        tiles.append(np.where(ok, _t5_bucket_np(di * dilation), -1))
    return np.stack(tiles).astype(np.int32)


def _bias_expand_kernel(tab_ref, bk_ref, o_ref):
    h = pl.program_id(0)
    bk = bk_ref[0]
    acc = jnp.full(bk.shape, NEG, F32)
    for b in range(REL_BUCKETS):
        acc = jnp.where(bk == b, tab_ref[b, h], acc)
    o_ref[0, 0] = acc


def _expand_bias(rel_bias, buckets):
    t, r, c = buckets.shape
    h = rel_bias.shape[1]
    return pl.pallas_call(
        _bias_expand_kernel,
        grid=(h, t),
        in_specs=[pl.BlockSpec(memory_space=pltpu.SMEM),
                  pl.BlockSpec((1, r, c), lambda hh, tt: (tt, 0, 0))],
        out_specs=pl.BlockSpec((1, 1, r, c), lambda hh, tt: (hh, tt, 0, 0)),
        out_shape=jax.ShapeDtypeStruct((h, t, r, c), F32),
        compiler_params=_cparams(2),
        name="bias_expand",
    )(rel_bias, jnp.asarray(buckets))


def _inproj_even_kernel(x_ref, g_ref, w_ref, gk_ref, main_ref, kd_ref, wi_ref, km_ref):
    xn = _rms(x_ref[...], g_ref[...]).astype(BF16)
    n_main = main_ref.shape[1]
    chunk = A_HEADS * HEAD_DIM
    for c in range(n_main // chunk):
        r = _dot(xn, w_ref[:, c * chunk:(c + 1) * chunk])
        main_ref[:, c * chunk:(c + 1) * chunk] = r.astype(BF16)
        if c == 1:
            km_ref[0] = jnp.mean(r, axis=0, keepdims=True)
    t = _dot(xn, w_ref[:, n_main:n_main + 2 * LANES])
    ki = t[:, :LANES]
    ms = jnp.mean(ki * ki, axis=-1, keepdims=True)
    kd_ref[...] = (ki * lax.rsqrt(ms + RMS_EPS) * gk_ref[...]).astype(BF16)
    wi_ref[...] = t[:, LANES:] * (IDX_HEADS ** -0.5 * IDX_DIM ** -0.5)


def _inproj_even(x2, g, w, gk):
    m, d = x2.shape
    n_main = 3 * A_HEADS * HEAD_DIM + 3 * B_HEADS * HEAD_DIM + IDX_HEADS * IDX_DIM
    assert ROW_TILE == MOBA_BLOCK and A_HEADS * HEAD_DIM == 512
    row = lambda i: (i, 0)
    const = lambda i: (0, 0)
    return pl.pallas_call(
        _inproj_even_kernel,
        grid=(m // ROW_TILE,),
        in_specs=[pl.BlockSpec((ROW_TILE, d), row), pl.BlockSpec((1, d), const),
                  pl.BlockSpec(w.shape, const), pl.BlockSpec((1, LANES), const)],
        out_specs=[pl.BlockSpec((ROW_TILE, n_main), row), pl.BlockSpec((ROW_TILE, LANES), row),
                   pl.BlockSpec((ROW_TILE, LANES), row),
                   pl.BlockSpec((1, 1, A_HEADS * HEAD_DIM), lambda i: (i, 0, 0))],
        out_shape=[jax.ShapeDtypeStruct((m, n_main), BF16), jax.ShapeDtypeStruct((m, LANES), BF16),
                   jax.ShapeDtypeStruct((m, LANES), F32),
                   jax.ShapeDtypeStruct((m // ROW_TILE, 1, A_HEADS * HEAD_DIM), F32)],
        compiler_params=_cparams(1),
        name="inproj_even",
    )(x2, g, w, gk)


def _inproj_odd_kernel(x_ref, g_ref, w_ref, o_ref):
    xn = _rms(x_ref[...], g_ref[...]).astype(BF16)
    n = o_ref.shape[1]
    chunk = 512
    for c in range(n // chunk):
        o_ref[:, c * chunk:(c + 1) * chunk] = _dot(xn, w_ref[:, c * chunk:(c + 1) * chunk]).astype(BF16)


def _inproj_odd(x2, g, w):
    m, d = x2.shape
    n = w.shape[1]
    return pl.pallas_call(
        _inproj_odd_kernel,
        grid=(m // ROW_TILE,),
        in_specs=[pl.BlockSpec((ROW_TILE, d), lambda i: (i, 0)), pl.BlockSpec((1, d), lambda i: (0, 0)),
                  pl.BlockSpec(w.shape, lambda i: (0, 0))],
        out_specs=pl.BlockSpec((ROW_TILE, n), lambda i: (i, 0)),
        out_shape=jax.ShapeDtypeStruct((m, n), BF16),
        compiler_params=_cparams(1),
        name="inproj_odd",
    )(x2, g, w)


def _softmax_step(carry, s, vj):
    m, l, acc = carry
    m_new = jnp.maximum(m, jnp.max(s, axis=-1, keepdims=True))
    alpha = jnp.exp(m - m_new)
    p = jnp.exp(s - m_new)
    l = alpha * l + jnp.sum(p, axis=-1, keepdims=True)
    acc = alpha * acc + _dot(p.astype(BF16), vj)
    return m_new, l, acc


def _softmax_init(tq):
    return (jnp.full((tq, 1), NEG, F32), jnp.zeros((tq, 1), F32), jnp.zeros((tq, LANES), F32))


def _head_lane_mask(shape, hh):
    lane = lax.broadcasted_iota(jnp.int32, shape, 1)
    return (lane >= hh * HEAD_DIM) & (lane < (hh + 1) * HEAD_DIM)


def _moba_kernel(q_ref, k_ref, v_ref, km_ref, e_ref, bias_ref, c31_ref, o_ref, *, n_sel):
    t = ATT_TILE
    pair = pl.program_id(1)
    own = pl.program_id(2)
    nb = km_ref.shape[1]
    q2 = q_ref[0] * SCALE
    lane = lax.broadcasted_iota(jnp.int32, (t, LANES), 1)
    km = jnp.concatenate([km_ref[0], jnp.zeros((LANES - nb, LANES), F32)], axis=0).astype(BF16)
    js = jnp.maximum(own - 1, 0)

    def rows(ref, j):
        return ref[0, pl.ds(pl.multiple_of(j * t, t), t), :]

    outs = []
    for hh in range(2):
        qm = jnp.where(_head_lane_mask((t, LANES), hh), q2, jnp.zeros_like(q2))
        gate = jnp.where(lane < own, _dot_t(qm, km), -jnp.inf)
        cnt = jnp.zeros((t, LANES), F32)
        for jp in range(nb):
            col = gate[:, jp:jp + 1]
            earlier = jnp.where(lane > jp, 1.0, 0.0)
            cnt = cnt + jnp.where(col > gate, 1.0, jnp.where(col == gate, earlier, 0.0))
        pen = jnp.where(lane == own, 0.0,
                        jnp.where(lane < own, jnp.where(cnt < n_sel, 0.0, NEG), NEG)).astype(BF16)
        c31 = c31_ref[2 * pair + hh]

        def far(j, carry, qm=qm, pen=pen, c31=c31):
            ej = e_ref[pl.ds(pl.multiple_of(j * t, t), t), :]
            s = _dot_t(qm, rows(k_ref, j)) + _dot_t(pen, ej) + c31
            return _softmax_step(carry, s, rows(v_ref, j))

        carry = lax.fori_loop(0, js, far, _softmax_init(t))
        ej = e_ref[pl.ds(pl.multiple_of(js * t, t), t), :]
        s = (_dot_t(qm, rows(k_ref, js)) + _dot_t(pen, ej) + bias_ref[hh, 0]
             + jnp.where(own == 0, NEG, 0.0))
        carry = _softmax_step(carry, s, rows(v_ref, js))
        s = _dot_t(qm, rows(k_ref, own)) + bias_ref[hh, 1]
        m, l, acc = _softmax_step(carry, s, rows(v_ref, own))
        outs.append(acc * (1.0 / l))
    o_ref[0] = jnp.where(lane < HEAD_DIM, outs[0], outs[1]).astype(BF16)


def _moba(main, kmean, e_onehot, bias_near, c31):
    b, s, _ = main.shape
    nb = s // MOBA_BLOCK
    assert MOBA_BLOCK == ATT_TILE and nb <= LANES
    pairs = A_HEADS // 2
    kern = functools.partial(_moba_kernel, n_sel=min(MOBA_TOPK, nb - 1))
    return pl.pallas_call(
        kern,
        grid=(b, pairs, nb),
        in_specs=[pl.BlockSpec((1, ATT_TILE, LANES), lambda bb, p, i: (bb, i, p)),
                  pl.BlockSpec((1, s, LANES), lambda bb, p, i: (bb, 0, pairs + p)),
                  pl.BlockSpec((1, s, LANES), lambda bb, p, i: (bb, 0, 2 * pairs + p)),
                  pl.BlockSpec((1, nb, LANES), lambda bb, p, i: (bb, 0, p)),
                  pl.BlockSpec((s, LANES), lambda bb, p, i: (0, 0)),
                  pl.BlockSpec((2, 2, ATT_TILE, ATT_TILE), lambda bb, p, i: (p, 0, 0, 0)),
                  pl.BlockSpec(memory_space=pltpu.SMEM)],
        out_specs=pl.BlockSpec((1, ATT_TILE, LANES), lambda bb, p, i: (bb, i, p)),
        out_shape=jax.ShapeDtypeStruct((b, s, A_HEADS * HEAD_DIM), BF16),
        compiler_params=_cparams(3),
        name="moba",
    )(main, main, main, kmean, e_onehot, bias_near, c31)


INT_MIN = np.int32(-2 ** 31)


def _key_to_f32(key):
    bits = jnp.where(key >= 0, key, key ^ jnp.int32(0x7FFFFFFF))
    return lax.bitcast_convert_type(bits, F32)


def _dsa_kernel(q_ref, k_ref, v_ref, qi_ref, kd_ref, w_ref, bias_ref, c31_ref, o_ref, sc_ref, wb_ref,
                *, n_keep, idx_bits):
    t = ATT_TILE
    qb = pl.program_id(1)
    n_tiles = qb + 1
    row = lax.broadcasted_iota(jnp.int32, (t, t), 0)
    col = lax.broadcasted_iota(jnp.int32, (t, t), 1)
    lane = lax.broadcasted_iota(jnp.int32, (t, LANES), 1)

    def wide(x):
        return jnp.concatenate([x] * (t // LANES), axis=1)

    def fold(x):
        return sum(x[:, i * LANES:(i + 1) * LANES] for i in range(t // LANES))

    w = w_ref[0]
    for h in range(IDX_HEADS):
        wb_ref[h] = jnp.broadcast_to(w[:, h:h + 1], (t, LANES))

    def score_tile(j, _):
        kd = kd_ref[0, pl.ds(pl.multiple_of(j * t, t), t), :]
        kd_half = [jnp.where(_head_lane_mask((t, LANES), hh), kd, jnp.zeros_like(kd)) for hh in range(2)]
        acc = jnp.zeros((t, t), F32)
        for h in range(IDX_HEADS):
            qp = qi_ref[0, :, (h // 2) * LANES:(h // 2 + 1) * LANES]
            r = _dot_t(qp, kd_half[h % 2])
            acc = acc + jnp.maximum(r, 0.0) * wide(wb_ref[h])
        past = jnp.where(j < qb, t, 0)
        sc_ref[j] = jnp.where(col <= row + past, acc, -jnp.inf)
        return 0

    lax.fori_loop(0, n_tiles, score_tile, 0)

    @pl.when(qb == 0)
    def _():
        sc_ref[0] = jnp.where(col <= row, 0.0, NEG)

    @pl.when(qb > 0)
    def _():
        def count(pred):
            def body(j, a):
                return a + fold(pred(sc_ref[j], j))
            return jnp.sum(lax.fori_loop(0, n_tiles, body, jnp.zeros((t, LANES), F32)), axis=1, keepdims=True)

        def count_ge(cand):
            cb = wide(jnp.broadcast_to(cand, (t, LANES)))
            return count(lambda x, j: jnp.where(x >= cb, 1.0, 0.0))

        def search(i, key):
            cand = key + lax.shift_left(jnp.int32(1), 31 - i)
            return jnp.where(count_ge(_key_to_f32(cand)) >= n_keep, cand, key)

        thr = _key_to_f32(lax.fori_loop(0, 32, search, jnp.full((t, 1), INT_MIN, jnp.int32)))
        tb = wide(jnp.broadcast_to(thr, (t, LANES)))
        n_ge = count_ge(thr)
        n_gt = count(lambda x, j: jnp.where(x > tb, 1.0, 0.0))
        need = n_keep - n_gt

        def tie_search():
            def step(i, lo):
                cand = lo + lax.shift_left(jnp.int32(1), idx_bits - 1 - i)
                cb = wide(jnp.broadcast_to(cand, (t, LANES)))
                below = count(lambda x, j: jnp.where(x == tb, jnp.where(j * t + col < cb, 1.0, 0.0), 0.0))
                return jnp.where(below < need, cand, lo)
            return lax.fori_loop(0, idx_bits, step, jnp.zeros((t, 1), jnp.int32))

        last = lax.cond(jnp.max(n_ge) > n_keep, tie_search,
                        lambda: jnp.full((t, 1), 2 ** idx_bits, jnp.int32))
        lb = wide(jnp.broadcast_to(last, (t, LANES)))

        def write(j, _):
            x = sc_ref[j]
            keep_tie = jnp.where(j * t + col <= lb, 0.0, NEG)
            sc_ref[j] = jnp.where(x > tb, 0.0, jnp.where(x == tb, keep_tie, NEG))
            return 0

        lax.fori_loop(0, n_tiles, write, 0)

    js = jnp.maximum(qb - 1, 0)
    for pr in range(B_HEADS // 2):
        sl = slice(pr * LANES, (pr + 1) * LANES)
        q2 = q_ref[0, :, sl] * SCALE

        def rows(ref, j, sl=sl):
            return ref[0, pl.ds(pl.multiple_of(j * t, t), t), sl]

        outs = []
        for hh in range(2):
            h = 2 * pr + hh
            qm = jnp.where(_head_lane_mask((t, LANES), hh), q2, jnp.zeros_like(q2))
            c31 = c31_ref[A_HEADS + h]

            def far(j, carry, qm=qm, c31=c31, rows=rows):
                s = _dot_t(qm, rows(k_ref, j)) + sc_ref[j] + c31
                return _softmax_step(carry, s, rows(v_ref, j))

            carry = lax.fori_loop(0, js, far, _softmax_init(t))
            s = (_dot_t(qm, rows(k_ref, js)) + sc_ref[js] + bias_ref[h, 0]
                 + jnp.where(qb == 0, NEG, 0.0))
            carry = _softmax_step(carry, s, rows(v_ref, js))
            s = _dot_t(qm, rows(k_ref, qb)) + sc_ref[qb] + bias_ref[h, 1]
            m, l, acc = _softmax_step(carry, s, rows(v_ref, qb))
            outs.append(acc * (1.0 / l))
        o_ref[0, :, sl] = jnp.where(lane < HEAD_DIM, outs[0], outs[1]).astype(BF16)


def _dsa(main, kd, wi, bias_near, c31):
    b, s, _ = main.shape
    nt = s // ATT_TILE
    assert s & (s - 1) == 0, "index tie-break search assumes a power-of-two sequence length"
    wq = B_HEADS * HEAD_DIM
    wi_cols = IDX_HEADS * IDX_DIM
    q_blk = (3 * A_HEADS * HEAD_DIM) // wq
    assert q_blk * wq == 3 * A_HEADS * HEAD_DIM and (q_blk + 3) * wq % wi_cols == 0
    qi_blk = (q_blk + 3) * wq // wi_cols
    kern = functools.partial(_dsa_kernel, n_keep=min(DSA_TOPK_MAX, s // 4), idx_bits=s.bit_length() - 1)
    return pl.pallas_call(
        kern,
        grid=(b, nt),
        in_specs=[pl.BlockSpec((1, ATT_TILE, wq), lambda bb, i: (bb, i, q_blk)),
                  pl.BlockSpec((1, s, wq), lambda bb, i: (bb, 0, q_blk + 1)),
                  pl.BlockSpec((1, s, wq), lambda bb, i: (bb, 0, q_blk + 2)),
                  pl.BlockSpec((1, ATT_TILE, wi_cols), lambda bb, i: (bb, i, qi_blk)),
                  pl.BlockSpec((1, s, LANES), lambda bb, i: (bb, 0, 0)),
                  pl.BlockSpec((1, ATT_TILE, LANES), lambda bb, i: (bb, i, 0)),
                  pl.BlockSpec((B_HEADS, 2, ATT_TILE, ATT_TILE), lambda bb, i: (1, 0, 0, 0)),
                  pl.BlockSpec(memory_space=pltpu.SMEM)],
        out_specs=pl.BlockSpec((1, ATT_TILE, wq), lambda bb, i: (bb, i, 0)),
        out_shape=jax.ShapeDtypeStruct((b, s, wq), BF16),
        scratch_shapes=[pltpu.VMEM((nt, ATT_TILE, ATT_TILE), F32),
                        pltpu.VMEM((IDX_HEADS, ATT_TILE, LANES), F32)],
        compiler_params=_cparams(2),
        name="dsa",
    )(main, main, main, main, kd, wi, bias_near, c31)


def _dilated_kernel(q_ref, kp_ref, kc_ref, vp_ref, vc_ref, bias_ref, o_ref, lse_ref):
    bb = q_ref.shape[1]
    first = pl.program_id(2) == 0
    lane = lax.broadcasted_iota(jnp.int32, (bb, LANES), 1)
    prev_pen = jnp.where(first, NEG, 0.0)
    lse_all = jnp.zeros((bb, LANES), F32)
    for pr in range(N_HEADS // 2):
        sl = slice(pr * LANES, (pr + 1) * LANES)
        q2 = q_ref[0, :, sl] * SCALE
        kp, kc, vp, vc = kp_ref[0, :, sl], kc_ref[0, :, sl], vp_ref[0, :, sl], vc_ref[0, :, sl]
        outs = []
        for hh in range(2):
            h = 2 * pr + hh
            qm = jnp.where(_head_lane_mask((bb, LANES), hh), q2, jnp.zeros_like(q2))
            sp = _dot_t(qm, kp) + bias_ref[h, :, :bb] + prev_pen
            sc = _dot_t(qm, kc) + bias_ref[h, :, bb:]
            m = jnp.maximum(jnp.max(sp, axis=-1, keepdims=True), jnp.max(sc, axis=-1, keepdims=True))
            pp = jnp.exp(sp - m)
            pc = jnp.exp(sc - m)
            l = jnp.sum(pp, axis=-1, keepdims=True) + jnp.sum(pc, axis=-1, keepdims=True)
            outs.append((_dot(pp.astype(BF16), vp) + _dot(pc.astype(BF16), vc)) * (1.0 / l))
            lse_all = jnp.where(lane == h, m + jnp.log(l), lse_all)
        o_ref[0, :, sl] = jnp.where(lane < HEAD_DIM, outs[0], outs[1]).astype(BF16)
    lse_ref[0] = lse_all


def _dilated(main, bias, window, dilation):
    b, s, c3 = main.shape
    c = c3 // 3
    bb = window // dilation
    n = s // dilation
    assert n % bb == 0 and n * dilation == s and bb % 8 == 0
    mv = main.reshape(b, n, dilation * c3)
    prev = lambda i: jnp.maximum(i - 1, 0)
    o, lse = pl.pallas_call(
        _dilated_kernel,
        grid=(b, dilation, n // bb),
        in_specs=[pl.BlockSpec((1, bb, c), lambda bi, r, i: (bi, i, 3 * r)),
                  pl.BlockSpec((1, bb, c), lambda bi, r, i: (bi, prev(i), 3 * r + 1)),
                  pl.BlockSpec((1, bb, c), lambda bi, r, i: (bi, i, 3 * r + 1)),
                  pl.BlockSpec((1, bb, c), lambda bi, r, i: (bi, prev(i), 3 * r + 2)),
                  pl.BlockSpec((1, bb, c), lambda bi, r, i: (bi, i, 3 * r + 2)),
                  pl.BlockSpec((N_HEADS, bb, 2 * bb), lambda bi, r, i: (0, 0, 0))],
        out_specs=[pl.BlockSpec((1, bb, c), lambda bi, r, i: (bi, i, r)),
                   pl.BlockSpec((1, bb, LANES), lambda bi, r, i: (bi, i, r))],
        out_shape=[jax.ShapeDtypeStruct((b, n, dilation * c), BF16),
                   jax.ShapeDtypeStruct((b, n, dilation * LANES), F32)],
        compiler_params=_cparams(3),
        name=f"dilated_d{dilation}",
    )(mv, mv, mv, mv, mv, bias)
    return o.reshape(b, s, c), lse.reshape(b, s, LANES)


def _outproj_even_kernel(oa_ref, ob_ref, x_ref, w_ref, out_ref):
    ka = oa_ref.shape[1]
    out_ref[...] = x_ref[...] + _dot(oa_ref[...], w_ref[:ka, :]) + _dot(ob_ref[...], w_ref[ka:, :])


def _outproj_even(oa, ob, x2, w):
    m, d = x2.shape
    row = lambda i: (i, 0)
    return pl.pallas_call(
        _outproj_even_kernel,
        grid=(m // ROW_TILE,),
        in_specs=[pl.BlockSpec((ROW_TILE, oa.shape[1]), row), pl.BlockSpec((ROW_TILE, ob.shape[1]), row),
                  pl.BlockSpec((ROW_TILE, d), row), pl.BlockSpec(w.shape, lambda i: (0, 0))],
        out_specs=pl.BlockSpec((ROW_TILE, d), row),
        out_shape=jax.ShapeDtypeStruct((m, d), F32),
        compiler_params=_cparams(1),
        name="outproj_even",
    )(oa, ob, x2, w)


def _mix_outproj_kernel(o1_ref, o2_ref, o3_ref, l1_ref, l2_ref, l3_ref, x_ref, w_ref, ex_ref, out_ref):
    ls = [l1_ref[...], l2_ref[...], l3_ref[...]]
    mx = jnp.maximum(jnp.maximum(ls[0], ls[1]), ls[2])
    es = [jnp.exp(l - mx) for l in ls]
    inv = 1.0 / (es[0] + es[1] + es[2])
    ex = ex_ref[...]
    mixed = None
    for e, o_ref in zip(es, (o1_ref, o2_ref, o3_ref)):
        a = e * inv
        a_hi = a.astype(BF16)
        a_lo = (a - a_hi.astype(F32)).astype(BF16)
        a_wide = _dot(a_hi, ex) + _dot(a_lo, ex)
        term = a_wide * o_ref[...].astype(F32)
        mixed = term if mixed is None else mixed + term
    out_ref[...] = x_ref[...] + _dot(mixed.astype(BF16), w_ref[...])


def _mix_outproj(os_, lses, x2, w, expand):
    m, d = x2.shape
    c = os_[0].shape[1]
    row = lambda i: (i, 0)
    const = lambda i: (0, 0)
    return pl.pallas_call(
        _mix_outproj_kernel,
        grid=(m // ROW_TILE,),
        in_specs=[pl.BlockSpec((ROW_TILE, c), row)] * 3 + [pl.BlockSpec((ROW_TILE, LANES), row)] * 3
        + [pl.BlockSpec((ROW_TILE, d), row), pl.BlockSpec(w.shape, const), pl.BlockSpec(expand.shape, const)],
        out_specs=pl.BlockSpec((ROW_TILE, d), row),
        out_shape=jax.ShapeDtypeStruct((m, d), F32),
        compiler_params=_cparams(1),
        name="mix_outproj",
    )(*os_, *lses, x2, w, expand)


FFN_CHUNK = 256
HALO = 8


def _ffn_kernel(x_ref, xh_ref, g_ref, wup_ref, cw_ref, cb_ref, wdn_ref, gf_ref, out_ref, act_ref, *, final_norm):
    tm = x_ref.shape[1]
    dff = wdn_ref.shape[0]
    x = x_ref[0]
    g = g_ref[...]
    xn = _rms(x, g).astype(BF16)
    xhn = _rms(xh_ref[0], g).astype(BF16)
    has_prev = jnp.where(pl.program_id(1) > 0, 1.0, 0.0)
    row = lax.broadcasted_iota(jnp.int32, (tm, FFN_CHUNK), 0)

    def conv(c0):
        wc = wup_ref[:, c0:c0 + FFN_CHUNK]
        u = _dot(xn, wc)
        uh = _dot(xhn, wc) * has_prev
        um1 = pltpu.roll(u, 1, axis=0)
        um2 = pltpu.roll(u, 2, axis=0)
        um1 = jnp.where(row == 0, uh[HALO - 1:HALO, :], um1)
        um2 = jnp.where(row == 0, uh[HALO - 2:HALO - 1, :], jnp.where(row == 1, uh[HALO - 1:HALO, :], um2))
        cw = cw_ref[:, c0:c0 + FFN_CHUNK]
        return cb_ref[:, c0:c0 + FFN_CHUNK] + cw[0:1] * um2 + cw[1:2] * um1 + cw[2:3] * u

    for c in range(dff // FFN_CHUNK):
        val = conv(c * FFN_CHUNK)
        gate = conv(dff + c * FFN_CHUNK)
        act_ref[:, c * FFN_CHUNK:(c + 1) * FFN_CHUNK] = (gate * (1.0 / (1.0 + jnp.exp(-gate))) * val).astype(BF16)
    y = x + _dot(act_ref[...], wdn_ref[...])
    if final_norm:
        y = _rms(y, gf_ref[...])
    out_ref[0] = y


def _ffn(x3, g, wup, cw, cb, wdn, gf, final_norm):
    b, s, d = x3.shape
    dff = wdn.shape[0]
    assert dff % FFN_CHUNK == 0 and ROW_TILE % HALO == 0 and CONV_WIDTH - 1 <= HALO
    const = lambda bi, i: (0, 0)
    per_tile = ROW_TILE // HALO
    return pl.pallas_call(
        functools.partial(_ffn_kernel, final_norm=final_norm),
        grid=(b, s // ROW_TILE),
        in_specs=[pl.BlockSpec((1, ROW_TILE, d), lambda bi, i: (bi, i, 0)),
                  pl.BlockSpec((1, HALO, d), lambda bi, i: (bi, jnp.maximum(i * per_tile - 1, 0), 0)),
                  pl.BlockSpec((1, d), const), pl.BlockSpec(wup.shape, const), pl.BlockSpec(cw.shape, const),
                  pl.BlockSpec(cb.shape, const), pl.BlockSpec(wdn.shape, const), pl.BlockSpec((1, d), const)],
        out_specs=pl.BlockSpec((1, ROW_TILE, d), lambda bi, i: (bi, i, 0)),
        out_shape=jax.ShapeDtypeStruct((b, s, d), F32),
        scratch_shapes=[pltpu.VMEM((ROW_TILE, dff), BF16)],
        compiler_params=_cparams(2),
        name="conv_ffn",
    )(x3, x3, g, wup, cw, cb, wdn, gf)


def kernel(x, w_in_even, idx_k_norm, w_in_odd, w_o, rel_bias, attn_norm, ffn_norm, w_up, conv_w, conv_b,
           w_down, final_norm):
    b, s, d = x.shape
    depth = w_o.shape[0]
    m = b * s
    assert s % ATT_TILE == 0 and d == N_HEADS * HEAD_DIM and A_HEADS + B_HEADS == N_HEADS

    bias_near = _expand_bias(rel_bias, _near_bucket_tiles(ATT_TILE))
    bias_dil = _expand_bias(rel_bias, _dilated_bucket_tiles())
    c31 = rel_bias[REL_BUCKETS - 1]
    far_from = int(np.argmax(_t5_bucket_np(np.arange(4 * REL_MAX_DIST)) == REL_BUCKETS - 1))
    assert far_from <= ATT_TILE

    blk = np.arange(s) // MOBA_BLOCK
    e_onehot = jnp.asarray(blk[:, None] == np.arange(LANES)[None, :], BF16)
    expand = jnp.asarray(np.arange(LANES)[:, None] == (np.arange(N_HEADS * HEAD_DIM) // HEAD_DIM)[None, :], BF16)

    n_qkv = 3 * A_HEADS * HEAD_DIM + 3 * B_HEADS * HEAD_DIM + IDX_HEADS * IDX_DIM
    x2 = x.reshape(m, d)
    for layer in range(depth):
        j = layer // 2
        g_attn = attn_norm[layer].reshape(1, d)
        wo = w_o[layer].astype(BF16)
        if layer % 2 == 0:
            w = w_in_even[j]
            w_ki = w[:, n_qkv:n_qkv + IDX_DIM]
            w_wi = w[:, n_qkv + IDX_DIM:]
            pad = jnp.zeros((d, LANES - IDX_HEADS), w.dtype)
            w_pad = jnp.concatenate([w[:, :n_qkv], w_ki, w_ki, w_wi, pad], axis=1).astype(BF16)
            gk = jnp.concatenate([idx_k_norm[j], idx_k_norm[j]]).reshape(1, LANES)
            main, kd, wi, kmean = _inproj_even(x2, g_attn, w_pad, gk)
            main = main.reshape(b, s, n_qkv)
            o_a = _moba(main, kmean.reshape(b, s // MOBA_BLOCK, A_HEADS * HEAD_DIM), e_onehot, bias_near, c31)
            o_b = _dsa(main, kd.reshape(b, s, LANES), wi.reshape(b, s, LANES), bias_near, c31)
            x2 = _outproj_even(o_a.reshape(m, -1), o_b.reshape(m, -1), x2, wo)
        else:
            main = _inproj_odd(x2, g_attn, w_in_odd[j].astype(BF16)).reshape(b, s, -1)
            outs, lses = [], []
            for p, (window, dilation) in enumerate(DILATED_PATTERNS):
                o, lse = _dilated(main, bias_dil[:, p], window, dilation)
                outs.append(o.reshape(m, -1))
                lses.append(lse.reshape(m, LANES))
            x2 = _mix_outproj(outs, lses, x2, wo, expand)
        last = layer == depth - 1
        x2 = _ffn(x2.reshape(b, s, d), ffn_norm[layer].reshape(1, d), w_up[layer].astype(BF16), conv_w[layer],
                  conv_b[layer].reshape(1, -1), w_down[layer].astype(BF16), final_norm.reshape(1, d),
                  final_norm=last).reshape(m, d)
    return x2.reshape(b, s, d)
```
